```python
import jax, jax.numpy as jnp
from jax import lax
import numpy as np

D_MODEL = 2048
BATCH = 2
SEQ = 8192
DEPTH = 1

CHUNK = 64
D_MIX = D_MODEL
D_CONV = D_MIX // 2
CONV_GROUPS = 8
CONV_WIDTH = 3
D_GLA_V = D_MIX - D_CONV
GLA_HEADS = 4
GLA_DV = D_GLA_V // GLA_HEADS
GLA_DK = GLA_DV // 2
D_GLA_K = GLA_HEADS * GLA_DK
GATE_RANK = 16
GATE_NORMALIZER = 16.0
D_FF = 4 * D_MODEL
EPS = 1e-6
IN_SIZES = (D_CONV, D_CONV, D_CONV, D_GLA_K, D_GLA_K, D_GLA_V, D_GLA_V, GATE_RANK)
D_IN = sum(IN_SIZES)

kernel_name = "hymba_conv_gla_sqrelu_block"


def rmsnorm(x, g):
    xf = x.astype(jnp.float32)
    y = xf * lax.rsqrt(jnp.mean(xf * xf, axis=-1, keepdims=True) + EPS)
    return (y * g.astype(jnp.float32)).astype(x.dtype)


def group_rms(x, groups):
    xf = x.astype(jnp.float32).reshape(x.shape[:-1] + (groups, x.shape[-1] // groups))
    xf = xf * lax.rsqrt(jnp.mean(xf * xf, axis=-1, keepdims=True) + EPS)
    return xf.reshape(x.shape)


def short_conv_mixer(b_gate, c_gate, h, conv_w, conv_g):
    seq = h.shape[1]
    u = c_gate * h
    up = jnp.pad(u, ((0, 0), (CONV_WIDTH - 1, 0), (0, 0)))
    conv = sum(up[:, k:k + seq, :] * conv_w[:, k] for k in range(CONV_WIDTH))
    y = b_gate * conv
    return (group_rms(y, CONV_GROUPS) * conv_g.astype(jnp.float32)).astype(h.dtype)


def gla_chunk_causal(q, k, v, log_a):
    bsz, seq, nh, dk = q.shape
    dv = v.shape[-1]
    nc = seq // CHUNK
    f32 = jnp.float32
    q = q.astype(f32).reshape(bsz, nc, CHUNK, nh, dk) * (dk ** -0.5)
    k = k.astype(f32).reshape(bsz, nc, CHUNK, nh, dk)
    v = v.astype(f32).reshape(bsz, nc, CHUNK, nh, dv)
    la = log_a.astype(f32).reshape(bsz, nc, CHUNK, nh, dk)
    b_cum = jnp.cumsum(la, axis=2)
    b_end = b_cum[:, :, -1:]
    k_dec = k * jnp.exp(b_end - b_cum)
    kv = jnp.einsum('bnshk,bnshv->bnhkv', k_dec, v)
    decay = jnp.exp(b_end[:, :, 0])

    def step(state, xs):
        q_c, dec_c, kv_c = xs
        state = dec_c[..., None] * state + kv_c
        o_c = jnp.einsum('bthk,bhkv->bthv', q_c, state)
        return state, o_c

    s0 = jnp.zeros((bsz, nh, dk, dv), f32)
    xs = (jnp.moveaxis(q, 1, 0), jnp.moveaxis(decay, 1, 0), jnp.moveaxis(kv, 1, 0))
    _, o = lax.scan(step, s0, xs)
    return jnp.moveaxis(o, 0, 1).reshape(bsz, seq, nh, dv)


def setup_inputs(seed: int = 0) -> dict:
    key = jax.random.key(seed)
    ks = jax.random.split(key, 14)
    f32 = jnp.float32
    nrm = lambda k, shape, s: jax.random.normal(k, shape, f32) * s
    gain = lambda k, shape: 1.0 + 0.02 * jax.random.normal(k, shape, f32)
    return {
        "x": jax.random.normal(ks[0], (BATCH, SEQ, D_MODEL), f32),
        "norm1_g": gain(ks[1], (DEPTH, D_MODEL)),
        "w_in": nrm(ks[2], (DEPTH, D_MODEL, D_IN), D_MODEL ** -0.5),
        "w_gate_up": nrm(ks[3], (DEPTH, GATE_RANK, D_GLA_K), GATE_RANK ** -0.5),
        "b_gate": nrm(ks[4], (DEPTH, D_GLA_K), 0.1),
        "conv_w": nrm(ks[5], (DEPTH, D_CONV, CONV_WIDTH), CONV_WIDTH ** -0.5),
        "conv_norm_g": gain(ks[6], (DEPTH, D_CONV)),
        "gla_norm_g": gain(ks[7], (DEPTH, GLA_DV)),
        "w_out": nrm(ks[8], (DEPTH, D_MIX, D_MODEL), D_MIX ** -0.5),
        "norm2_g": gain(ks[9], (DEPTH, D_MODEL)),
        "w_ff1": nrm(ks[10], (DEPTH, D_MODEL, D_FF), D_MODEL ** -0.5),
        "w_ff2": nrm(ks[11], (DEPTH, D_FF, D_MODEL), D_FF ** -0.5),
        "norm_f_g": gain(ks[12], (D_MODEL,)),
    }


def reference(x, norm1_g, w_in, w_gate_up, b_gate, conv_w, conv_norm_g, gla_norm_g,
              w_out, norm2_g, w_ff1, w_ff2, norm_f_g):
    bsz, seq, _ = x.shape
    split_at = [int(i) for i in np.cumsum(IN_SIZES)[:-1]]
    for l in range(DEPTH):
        u = rmsnorm(x, norm1_g[l])
        z = u @ w_in[l]
        cb, cc, ch, q, k, v, og, a_low = jnp.split(z, split_at, axis=-1)
        y_conv = short_conv_mixer(cb, cc, ch, conv_w[l], conv_norm_g[l])
        log_a = jax.nn.log_sigmoid(a_low @ w_gate_up[l] + b_gate[l]) / GATE_NORMALIZER
        o = gla_chunk_causal(q.reshape(bsz, seq, GLA_HEADS, GLA_DK),
                             k.reshape(bsz, seq, GLA_HEADS, GLA_DK),
                             v.reshape(bsz, seq, GLA_HEADS, GLA_DV),
                             log_a.reshape(bsz, seq, GLA_HEADS, GLA_DK))
        o = o * lax.rsqrt(jnp.mean(o * o, axis=-1, keepdims=True) + EPS)
        o = o * gla_norm_g[l].astype(jnp.float32) * jax.nn.silu(
            og.astype(jnp.float32).reshape(bsz, seq, GLA_HEADS, GLA_DV))
        y_gla = o.reshape(bsz, seq, D_GLA_V).astype(x.dtype)
        y = jnp.concatenate([y_conv, y_gla], axis=-1)
        x = x + y @ w_out[l]
        h = rmsnorm(x, norm2_g[l])
        x = x + jnp.square(jax.nn.relu(h @ w_ff1[l])) @ w_ff2[l]
    return rmsnorm(x, norm_f_g)
```

```python
import functools

import numpy as np
import jax
import jax.numpy as jnp
from jax import lax
from jax.experimental import pallas as pl
from jax.experimental.pallas import tpu as pltpu

F32 = jnp.float32
BF16 = jnp.bfloat16

EPS = 1e-6
CHUNK = 64
CONV_GROUPS = 8
CONV_WIDTH = 3
GLA_HEADS = 4
GATE_RANK = 16
GATE_NORMALIZER = 16.0

LANES = 128
SUBLANES = 8
MIB = 1024 * 1024

TM_IN = 1024
TN_IN = 896
TM_MIX = 256
TM_MLP = 1024
TF_MLP = 512
ROWS_NORM = 64


def _rms_rows(x, g):
    ms = jnp.mean(x * x, axis=-1, keepdims=True)
    return x * lax.rsqrt(ms + EPS) * g


def _inproj_kernel(x_ref, g_ref, w_ref, z_ref, u_ref):
    @pl.when(pl.program_id(1) == 0)
    def _():
        g = g_ref[...]

        def body(i, c):
            r = pl.ds(pl.multiple_of(i * ROWS_NORM, ROWS_NORM), ROWS_NORM)
            u_ref[r, :] = _rms_rows(x_ref[r, :], g).astype(BF16)
            return c

        lax.fori_loop(0, x_ref.shape[0] // ROWS_NORM, body, 0)

    z_ref[...] = jnp.dot(u_ref[...], w_ref[...], preferred_element_type=F32).astype(z_ref.dtype)


def _inproj(x2d, g, w_p):
    t, d = x2d.shape
    n = w_p.shape[1]
    vmem = (2 * TM_IN * d * 4 + TM_IN * d * 2 + 2 * d * TN_IN * 2 + 2 * TM_IN * TN_IN * 2
            + TM_IN * TN_IN * 4 + 8 * MIB)
    return pl.pallas_call(
        _inproj_kernel,
        grid=(t // TM_IN, n // TN_IN),
        in_specs=[
            pl.BlockSpec((TM_IN, d), lambda i, j: (i, 0)),
            pl.BlockSpec((1, d), lambda i, j: (0, 0)),
            pl.BlockSpec((d, TN_IN), lambda i, j: (0, j)),
        ],
        out_specs=pl.BlockSpec((TM_IN, TN_IN), lambda i, j: (i, j)),
        out_shape=jax.ShapeDtypeStruct((t, n), BF16),
        scratch_shapes=[pltpu.VMEM((TM_IN, d), BF16)],
        compiler_params=pltpu.CompilerParams(
            dimension_semantics=("arbitrary", "arbitrary"), vmem_limit_bytes=vmem),
        name="inproj",
    )(x2d, g, w_p)


def _mixer_kernel(z_ref, x_ref, wg_ref, bg_ref, cw_ref, cg_ref, gg_ref, wo_ref, us_ref,
                  o_ref, y_ref, st_ref, tail_ref, *, d_conv, d_k, d_v):
    tm = x_ref.shape[0]
    nc = tm // CHUNK
    dk = d_k // GLA_HEADS
    dv = d_v // GLA_HEADS
    oq = 3 * d_conv
    ok = oq + d_k
    ov = ok + d_k
    og = ov + d_v
    oa = og + d_v
    gw = d_conv // CONV_GROUPS

    @pl.when(pl.program_id(1) == 0)
    def _():
        st_ref[...] = jnp.zeros_like(st_ref)
        tail_ref[...] = jnp.zeros_like(tail_ref)

    w0 = cw_ref[0:1, :]
    w1 = cw_ref[1:2, :]
    w2 = cw_ref[2:3, :]
    prev = tail_ref[...]
    for c in range(nc):
        r = slice(c * CHUNK, (c + 1) * CHUNK)
        u = z_ref[r, d_conv:2 * d_conv].astype(F32) * z_ref[r, 2 * d_conv:3 * d_conv].astype(F32)
        ext = jnp.concatenate([prev, u], axis=0)
        u1 = pltpu.roll(ext, 1, axis=0)[SUBLANES:, :]
        u2 = pltpu.roll(ext, 2, axis=0)[SUBLANES:, :]
        conv = u2 * w0 + u1 * w1 + u * w2
        y = z_ref[r, 0:d_conv].astype(F32) * conv
        for g in range(CONV_GROUPS):
            cs = slice(g * gw, (g + 1) * gw)
            yg = y[:, cs]
            ms = jnp.mean(yg * yg, axis=-1, keepdims=True)
            y_ref[r, cs] = (yg * lax.rsqrt(ms + EPS) * cg_ref[:, cs]).astype(BF16)
        prev = u[CHUNK - SUBLANES:, :]
    tail_ref[...] = prev

    gate = jnp.dot(z_ref[:, oa:oa + LANES], wg_ref[...], preferred_element_type=F32) + bg_ref[...]
    la = (jnp.minimum(gate, 0.0) - jnp.log1p(jnp.exp(-jnp.abs(gate)))) * (1.0 / GATE_NORMALIZER)
    la_hi = la.astype(BF16)
    la_lo = (la - la_hi.astype(F32)).astype(BF16)
    us = us_ref[...]
    suf = (jnp.dot(us, la_hi, preferred_element_type=F32)
           + jnp.dot(us, la_lo, preferred_element_type=F32))
    dec = jnp.exp(suf)

    scale = dk ** -0.5
    gg = gg_ref[...]
    for c in range(nc):
        r = slice(c * CHUNK, (c + 1) * CHUNK)
        b_end = suf[c * CHUNK:c * CHUNK + 1, :] + la[c * CHUNK:c * CHUNK + 1, :]
        a_end = jnp.exp(b_end)
        for h in range(GLA_HEADS):
            ks = slice(h * dk, (h + 1) * dk)
            kd = (z_ref[r, ok + h * dk:ok + (h + 1) * dk].astype(F32) * dec[r, ks]).astype(BF16)
            vh = z_ref[r, ov + h * dv:ov + (h + 1) * dv]
            kvt = lax.dot_general(vh, kd, (((0,), (0,)), ((), ())), preferred_element_type=F32)
            st = st_ref[h] * a_end[:, ks] + kvt
            st_ref[h] = st
            qs = (z_ref[r, oq + h * dk:oq + (h + 1) * dk].astype(F32) * scale).astype(BF16)
            o = lax.dot_general(qs, st.astype(BF16), (((1,), (1,)), ((), ())),
                                preferred_element_type=F32)
            ms = jnp.mean(o * o, axis=-1, keepdims=True)
            gt = z_ref[r, og + h * dv:og + (h + 1) * dv].astype(F32)
            silu = gt / (1.0 + jnp.exp(-gt))
            y_ref[r, d_conv + h * dv:d_conv + (h + 1) * dv] = (
                o * lax.rsqrt(ms + EPS) * gg * silu).astype(BF16)

    o_ref[...] = x_ref[...] + jnp.dot(y_ref[...], wo_ref[...], preferred_element_type=F32)


def _mixer(z3, x, wg_p, bg, cw, cg, gg, wo, us, *, d_conv, d_k, d_v):
    b, s, d = x.shape
    nz = z3.shape[-1]
    d_mix = wo.shape[0]
    dk = d_k // GLA_HEADS
    dv = d_v // GLA_HEADS
    const = lambda shape: pl.BlockSpec(shape, lambda bi, ti: (0,) * len(shape))
    vmem = (2 * TM_MIX * nz * 2 + 4 * TM_MIX * d * 4 + 2 * d_mix * d * 2 + TM_MIX * d_mix * 2
            + GLA_HEADS * dv * dk * 4 + 16 * MIB)
    kern = functools.partial(_mixer_kernel, d_conv=d_conv, d_k=d_k, d_v=d_v)
    return pl.pallas_call(
        kern,
        grid=(b, s // TM_MIX),
        in_specs=[
            pl.BlockSpec((None, TM_MIX, nz), lambda bi, ti: (bi, ti, 0)),
            pl.BlockSpec((None, TM_MIX, d), lambda bi, ti: (bi, ti, 0)),
            const(wg_p.shape), const(bg.shape), const(cw.shape), const(cg.shape), const(gg.shape),
            const(wo.shape), const(us.shape),
        ],
        out_specs=pl.BlockSpec((None, TM_MIX, d), lambda bi, ti: (bi, ti, 0)),
        out_shape=jax.ShapeDtypeStruct((b, s, d), F32),
        scratch_shapes=[
            pltpu.VMEM((TM_MIX, d_mix), BF16),
            pltpu.VMEM((GLA_HEADS, dv, dk), F32),
            pltpu.VMEM((SUBLANES, d_conv), F32),
        ],
        compiler_params=pltpu.CompilerParams(
            dimension_semantics=("arbitrary", "arbitrary"), vmem_limit_bytes=vmem),
        name="mixer",
    )(z3, x, wg_p, bg, cw, cg, gg, wo, us)


def _mlp_kernel(x_ref, g2_ref, w1_ref, w2_ref, gf_ref, o_ref, h_ref):
    f = pl.program_id(1)
    nrow = x_ref.shape[0] // ROWS_NORM

    @pl.when(f == 0)
    def _():
        g = g2_ref[...]

        def body(i, c):
            r = pl.ds(pl.multiple_of(i * ROWS_NORM, ROWS_NORM), ROWS_NORM)
            h_ref[r, :] = _rms_rows(x_ref[r, :], g).astype(BF16)
            return c

        lax.fori_loop(0, nrow, body, 0)

    a = jnp.dot(h_ref[...], w1_ref[...], preferred_element_type=F32)
    a = jnp.maximum(a, 0.0)
    contrib = jnp.dot((a * a).astype(BF16), w2_ref[...], preferred_element_type=F32)

    @pl.when(f == 0)
    def _():
        o_ref[...] = contrib

    @pl.when(f != 0)
    def _():
        o_ref[...] += contrib

    @pl.when(f == pl.num_programs(1) - 1)
    def _():
        g = gf_ref[...]

        def body(i, c):
            r = pl.ds(pl.multiple_of(i * ROWS_NORM, ROWS_NORM), ROWS_NORM)
            o_ref[r, :] = _rms_rows(x_ref[r, :] + o_ref[r, :], g)
            return c

        lax.fori_loop(0, nrow, body, 0)


def _mlp(x2d, g2, w1, w2, gf):
    t, d = x2d.shape
    dff = w1.shape[1]
    vmem = (4 * TM_MLP * d * 4 + TM_MLP * d * 2 + 4 * d * TF_MLP * 2
            + TM_MLP * TF_MLP * 6 + 8 * MIB)
    return pl.pallas_call(
        _mlp_kernel,
        grid=(t // TM_MLP, dff // TF_MLP),
        in_specs=[
            pl.BlockSpec((TM_MLP, d), lambda i, f: (i, 0)),
            pl.BlockSpec((1, d), lambda i, f: (0, 0)),
            pl.BlockSpec((d, TF_MLP), lambda i, f: (0, f)),
            pl.BlockSpec((TF_MLP, d), lambda i, f: (f, 0)),
            pl.BlockSpec((1, d), lambda i, f: (0, 0)),
        ],
        out_specs=pl.BlockSpec((TM_MLP, d), lambda i, f: (i, 0)),
        out_shape=jax.ShapeDtypeStruct((t, d), F32),
        scratch_shapes=[pltpu.VMEM((TM_MLP, d), BF16)],
        compiler_params=pltpu.CompilerParams(
            dimension_semantics=("arbitrary", "arbitrary"), vmem_limit_bytes=vmem),
        name="mlp",
    )(x2d, g2, w1, w2, gf)


def _suffix_mask(tm):
    t = np.arange(tm)
    m = (t[:, None] // CHUNK == t[None, :] // CHUNK) & (t[None, :] > t[:, None])
    return jnp.asarray(m, dtype=BF16)


def kernel(x, norm1_g, w_in, w_gate_up, b_gate, conv_w, conv_norm_g, gla_norm_g, w_out, norm2_g,
           w_ff1, w_ff2, norm_f_g):
    b, s, d = x.shape
    depth = w_in.shape[0]
    d_conv = conv_w.shape[1]
    d_k = w_gate_up.shape[2]
    d_v = gla_norm_g.shape[1] * GLA_HEADS
    n_main = 3 * d_conv + 2 * d_k + 2 * d_v
    assert w_in.shape[2] == n_main + GATE_RANK
    us = _suffix_mask(TM_MIX)
    for l in range(depth):
        w_p = jnp.pad(w_in[l].astype(BF16), ((0, 0), (0, LANES - GATE_RANK)))
        wg_p = jnp.pad(w_gate_up[l].astype(BF16), ((0, LANES - GATE_RANK), (0, 0)))
        z = _inproj(x.reshape(b * s, d), norm1_g[l][None, :], w_p)
        x = _mixer(z.reshape(b, s, -1), x, wg_p, b_gate[l][None, :], conv_w[l].T,
                   conv_norm_g[l][None, :], gla_norm_g[l][None, :], w_out[l].astype(BF16), us,
                   d_conv=d_conv, d_k=d_k, d_v=d_v)
        gf = norm_f_g[None, :] if l == depth - 1 else None
        assert gf is not None, "final norm is fused into the last layer's MLP call"
        x = _mlp(x.reshape(b * s, d), norm2_g[l][None, :], w_ff1[l].astype(BF16),
                 w_ff2[l].astype(BF16), gf).reshape(b, s, d)
    return x
```

```python
import functools

import numpy as np
import jax
import jax.numpy as jnp
from jax import lax
from jax.experimental import pallas as pl
from jax.experimental.pallas import tpu as pltpu

F32 = jnp.float32
BF16 = jnp.bfloat16

EPS = 1e-6
CHUNK = 64
CONV_GROUPS = 8
CONV_WIDTH = 3
GLA_HEADS = 4
GATE_RANK = 16
GATE_NORMALIZER = 16.0

LANES = 128
SUBLANES = 8
MXU_COLS = 256
MIB = 1024 * 1024

TM_IN = 1024
TN_IN = 1280
TM_MIX = 256
TM_MLP = 1024
TF_MLP = 512
ROWS_NORM = 64


def _rms_rows(x, g):
    ms = jnp.mean(x * x, axis=-1, keepdims=True)
    return x * lax.rsqrt(ms + EPS) * g


def _inproj_kernel(x_ref, g_ref, w_ref, z_ref, u_ref):
    @pl.when(pl.program_id(1) == 0)
    def _():
        g = g_ref[...]

        def body(i, c):
            r = pl.ds(pl.multiple_of(i * ROWS_NORM, ROWS_NORM), ROWS_NORM)
            u_ref[r, :] = _rms_rows(x_ref[r, :], g).astype(BF16)
            return c

        lax.fori_loop(0, x_ref.shape[0] // ROWS_NORM, body, 0)

    z_ref[...] = jnp.dot(u_ref[...], w_ref[...], preferred_element_type=F32).astype(z_ref.dtype)


def _inproj(x2d, g, w_p):
    t, d = x2d.shape
    n = w_p.shape[1]
    vmem = (2 * TM_IN * d * 4 + TM_IN * d * 2 + 2 * d * TN_IN * 2 + 2 * TM_IN * TN_IN * 2
            + TM_IN * TN_IN * 4 + 8 * MIB)
    return pl.pallas_call(
        _inproj_kernel,
        grid=(t // TM_IN, n // TN_IN),
        in_specs=[
            pl.BlockSpec((TM_IN, d), lambda i, j: (i, 0)),
            pl.BlockSpec((1, d), lambda i, j: (0, 0)),
            pl.BlockSpec((d, TN_IN), lambda i, j: (0, j)),
        ],
        out_specs=pl.BlockSpec((TM_IN, TN_IN), lambda i, j: (i, j)),
        out_shape=jax.ShapeDtypeStruct((t, n), BF16),
        scratch_shapes=[pltpu.VMEM((TM_IN, d), BF16)],
        compiler_params=pltpu.CompilerParams(
            dimension_semantics=("arbitrary", "arbitrary"), vmem_limit_bytes=vmem),
        name="inproj",
    )(x2d, g, w_p)


def _mixer_kernel(z_ref, x_ref, wg_ref, bg_ref, cw_ref, cg_ref, gg_ref, wo_ref, us_ref,
                  o_ref, y_ref, yp_ref, st_ref, tail_ref, *, d_conv, d_k, d_v):
    tm = x_ref.shape[0]
    ti = pl.program_id(1)
    nc = tm // CHUNK
    dk = d_k // GLA_HEADS
    dv = d_v // GLA_HEADS
    oq = 3 * d_conv
    ok = oq + d_k
    ov = ok + d_k
    og = ov + d_v
    oa = og + d_v
    gw = d_conv // CONV_GROUPS

    @pl.when(ti == 0)
    def _():
        st_ref[...] = jnp.zeros_like(st_ref)
        tail_ref[...] = jnp.zeros_like(tail_ref)
        yp_ref[...] = jnp.zeros_like(yp_ref)

    w0 = cw_ref[0:1, :]
    w1 = cw_ref[1:2, :]
    w2 = cw_ref[2:3, :]
    prev = tail_ref[...]
    for c in range(nc):
        r = slice(c * CHUNK, (c + 1) * CHUNK)
        u = z_ref[r, d_conv:2 * d_conv].astype(F32) * z_ref[r, 2 * d_conv:3 * d_conv].astype(F32)
        ext = jnp.concatenate([prev, u], axis=0)
        u1 = pltpu.roll(ext, 1, axis=0)[SUBLANES:, :]
        u2 = pltpu.roll(ext, 2, axis=0)[SUBLANES:, :]
        conv = u2 * w0 + u1 * w1 + u * w2
        y = z_ref[r, 0:d_conv].astype(F32) * conv
        for g in range(CONV_GROUPS):
            cs = slice(g * gw, (g + 1) * gw)
            yg = y[:, cs]
            ms = jnp.mean(yg * yg, axis=-1, keepdims=True)
            y_ref[r, cs] = (yg * lax.rsqrt(ms + EPS) * cg_ref[:, cs]).astype(BF16)
        prev = u[CHUNK - SUBLANES:, :]
    tail_ref[...] = prev

    gate = jnp.dot(z_ref[:, oa:oa + LANES], wg_ref[...], preferred_element_type=F32) + bg_ref[...]
    la = (jnp.minimum(gate, 0.0) - jnp.log1p(jnp.exp(-jnp.abs(gate)))) * (1.0 / GATE_NORMALIZER)
    la_hi = la.astype(BF16)
    la_lo = (la - la_hi.astype(F32)).astype(BF16)
    us = us_ref[...]
    suf = (jnp.dot(us, la_hi, preferred_element_type=F32)
           + jnp.dot(us, la_lo, preferred_element_type=F32))
    dec = jnp.exp(suf)

    kvts = []
    for c in range(nc):
        r = slice(c * CHUNK, (c + 1) * CHUNK)
        row = []
        for h in range(GLA_HEADS):
            ks = slice(h * dk, (h + 1) * dk)
            kd = (z_ref[r, ok + h * dk:ok + (h + 1) * dk].astype(F32) * dec[r, ks]).astype(BF16)
            vh = z_ref[r, ov + h * dv:ov + (h + 1) * dv]
            row.append(lax.dot_general(vh, kd, (((0,), (0,)), ((), ())),
                                       preferred_element_type=F32))
        kvts.append(row)

    o_ref[...] = x_ref[...] + jnp.dot(yp_ref[...], wo_ref[...], preferred_element_type=F32)

    sts = []
    st = [st_ref[h] for h in range(GLA_HEADS)]
    for c in range(nc):
        b_end = suf[c * CHUNK:c * CHUNK + 1, :] + la[c * CHUNK:c * CHUNK + 1, :]
        a_end = jnp.exp(b_end)
        st = [st[h] * a_end[:, h * dk:(h + 1) * dk] + kvts[c][h] for h in range(GLA_HEADS)]
        sts.append([s.astype(BF16) for s in st])
    for h in range(GLA_HEADS):
        st_ref[h] = st[h]

    scale = dk ** -0.5
    gg = gg_ref[...]
    for c in range(nc):
        r = slice(c * CHUNK, (c + 1) * CHUNK)
        for h in range(GLA_HEADS):
            qs = (z_ref[r, oq + h * dk:oq + (h + 1) * dk].astype(F32) * scale).astype(BF16)
            o = lax.dot_general(qs, sts[c][h], (((1,), (1,)), ((), ())),
                                preferred_element_type=F32)
            ms = jnp.mean(o * o, axis=-1, keepdims=True)
            gt = z_ref[r, og + h * dv:og + (h + 1) * dv].astype(F32)
            silu = gt / (1.0 + jnp.exp(-gt))
            y_ref[r, d_conv + h * dv:d_conv + (h + 1) * dv] = (
                o * lax.rsqrt(ms + EPS) * gg * silu).astype(BF16)

    yp_ref[...] = y_ref[...]


def _mixer(z3, x, wg_p, bg, cw, cg, gg, wo, us, *, d_conv, d_k, d_v):
    b, s, d = x.shape
    nz = z3.shape[-1]
    d_mix = wo.shape[0]
    dk = d_k // GLA_HEADS
    dv = d_v // GLA_HEADS
    const = lambda shape: pl.BlockSpec(shape, lambda bi, ti: (0,) * len(shape))
    vmem = (2 * TM_MIX * nz * 2 + 4 * TM_MIX * d * 4 + 2 * d_mix * d * 2 + 2 * TM_MIX * d_mix * 2
            + GLA_HEADS * dv * dk * 4 + 16 * MIB)
    kern = functools.partial(_mixer_kernel, d_conv=d_conv, d_k=d_k, d_v=d_v)
    nt = s // TM_MIX
    cur = lambda bi, ti: (bi, jnp.minimum(ti, nt - 1), 0)
    prv = lambda bi, ti: (bi, jnp.maximum(ti - 1, 0), 0)
    return pl.pallas_call(
        kern,
        grid=(b, nt + 1),
        in_specs=[
            pl.BlockSpec((None, TM_MIX, nz), cur),
            pl.BlockSpec((None, TM_MIX, d), prv),
            const(wg_p.shape), const(bg.shape), const(cw.shape), const(cg.shape), const(gg.shape),
            const(wo.shape), const(us.shape),
        ],
        out_specs=pl.BlockSpec((None, TM_MIX, d), prv),
        out_shape=jax.ShapeDtypeStruct((b, s, d), F32),
        scratch_shapes=[
            pltpu.VMEM((TM_MIX, d_mix), BF16),
            pltpu.VMEM((TM_MIX, d_mix), BF16),
            pltpu.VMEM((GLA_HEADS, dv, dk), F32),
            pltpu.VMEM((SUBLANES, d_conv), F32),
        ],
        compiler_params=pltpu.CompilerParams(
            dimension_semantics=("arbitrary", "arbitrary"), vmem_limit_bytes=vmem),
        name="mixer",
    )(z3, x, wg_p, bg, cw, cg, gg, wo, us)


def _mlp_kernel(x_ref, g2_ref, w1_ref, w2_ref, gf_ref, o_ref, h_ref):
    f = pl.program_id(1)
    nrow = x_ref.shape[0] // ROWS_NORM

    @pl.when(f == 0)
    def _():
        g = g2_ref[...]

        def body(i, c):
            r = pl.ds(pl.multiple_of(i * ROWS_NORM, ROWS_NORM), ROWS_NORM)
            x = x_ref[r, :]
            h_ref[r, :] = _rms_rows(x, g).astype(BF16)
            o_ref[r, :] = x
            return c

        lax.fori_loop(0, nrow, body, 0)

    a = jnp.dot(h_ref[...], w1_ref[...], preferred_element_type=F32)
    a = jnp.maximum(a, 0.0)
    o_ref[...] += jnp.dot((a * a).astype(BF16), w2_ref[...], preferred_element_type=F32)

    @pl.when(f == pl.num_programs(1) - 1)
    def _():
        g = gf_ref[...]

        def body(i, c):
            r = pl.ds(pl.multiple_of(i * ROWS_NORM, ROWS_NORM), ROWS_NORM)
            o_ref[r, :] = _rms_rows(o_ref[r, :], g)
            return c

        lax.fori_loop(0, nrow, body, 0)


def _mlp(x2d, g2, w1, w2, gf):
    t, d = x2d.shape
    dff = w1.shape[1]
    vmem = (4 * TM_MLP * d * 4 + TM_MLP * d * 2 + 4 * d * TF_MLP * 2
            + TM_MLP * TF_MLP * 6 + 8 * MIB)
    return pl.pallas_call(
        _mlp_kernel,
        grid=(t // TM_MLP, dff // TF_MLP),
        in_specs=[
            pl.BlockSpec((TM_MLP, d), lambda i, f: (i, 0)),
            pl.BlockSpec((1, d), lambda i, f: (0, 0)),
            pl.BlockSpec((d, TF_MLP), lambda i, f: (0, f)),
            pl.BlockSpec((TF_MLP, d), lambda i, f: (f, 0)),
            pl.BlockSpec((1, d), lambda i, f: (0, 0)),
        ],
        out_specs=pl.BlockSpec((TM_MLP, d), lambda i, f: (i, 0)),
        out_shape=jax.ShapeDtypeStruct((t, d), F32),
        scratch_shapes=[pltpu.VMEM((TM_MLP, d), BF16)],
        compiler_params=pltpu.CompilerParams(
            dimension_semantics=("arbitrary", "arbitrary"), vmem_limit_bytes=vmem),
        name="mlp",
    )(x2d, g2, w1, w2, gf)


def _suffix_mask(tm):
    t = np.arange(tm)
    m = (t[:, None] // CHUNK == t[None, :] // CHUNK) & (t[None, :] > t[:, None])
    return jnp.asarray(m, dtype=BF16)


def kernel(x, norm1_g, w_in, w_gate_up, b_gate, conv_w, conv_norm_g, gla_norm_g, w_out, norm2_g,
           w_ff1, w_ff2, norm_f_g):
    b, s, d = x.shape
    depth = w_in.shape[0]
    d_conv = conv_w.shape[1]
    d_k = w_gate_up.shape[2]
    d_v = gla_norm_g.shape[1] * GLA_HEADS
    n_main = 3 * d_conv + 2 * d_k + 2 * d_v
    assert w_in.shape[2] == n_main + GATE_RANK
    us = _suffix_mask(TM_MIX)
    for l in range(depth):
        w_p = jnp.pad(w_in[l].astype(BF16), ((0, 0), (0, MXU_COLS - GATE_RANK)))
        wg_p = jnp.pad(w_gate_up[l].astype(BF16), ((0, LANES - GATE_RANK), (0, 0)))
        z = _inproj(x.reshape(b * s, d), norm1_g[l][None, :], w_p)
        x = _mixer(z.reshape(b, s, -1), x, wg_p, b_gate[l][None, :], conv_w[l].T,
                   conv_norm_g[l][None, :], gla_norm_g[l][None, :], w_out[l].astype(BF16), us,
                   d_conv=d_conv, d_k=d_k, d_v=d_v)
        gf = norm_f_g[None, :] if l == depth - 1 else None
        assert gf is not None, "final norm is fused into the last layer's MLP call"
        x = _mlp(x.reshape(b * s, d), norm2_g[l][None, :], w_ff1[l].astype(BF16),
                 w_ff2[l].astype(BF16), gf).reshape(b, s, d)
    return x
```

```python
import functools

import numpy as np
import jax
import jax.numpy as jnp
from jax import lax
from jax.experimental import pallas as pl
from jax.experimental.pallas import tpu as pltpu

F32 = jnp.float32
BF16 = jnp.bfloat16

EPS = 1e-6
CHUNK = 64
CONV_GROUPS = 8
CONV_WIDTH = 3
GLA_HEADS = 4
GATE_RANK = 16
GATE_NORMALIZER = 16.0

LANES = 128
SUBLANES = 8
MXU_COLS = 256
MIB = 1024 * 1024

TM_IN = 1024
TN_IN = 768
TM_MIX = 512
TM_MLP = 1024
TF_MLP = 512
ROWS_NORM = 64
OUT_PARTS = 4


def _rms_rows(x, g):
    ms = jnp.mean(x * x, axis=-1, keepdims=True)
    return x * lax.rsqrt(ms + EPS) * g


def _inproj_kernel(x_ref, g_ref, w_ref, wa_ref, w1_ref, w2_ref, wo_ref,
                   z_ref, za_ref, w1b_ref, w2b_ref, wob_ref, u_ref):
    @pl.when(pl.program_id(1) == 0)
    def _():
        g = g_ref[...]

        def body(i, c):
            r = pl.ds(pl.multiple_of(i * ROWS_NORM, ROWS_NORM), ROWS_NORM)
            u_ref[r, :] = _rms_rows(x_ref[r, :], g).astype(BF16)
            return c

        lax.fori_loop(0, x_ref.shape[0] // ROWS_NORM, body, 0)
        za_ref[...] = jnp.dot(u_ref[...], wa_ref[...], preferred_element_type=F32).astype(za_ref.dtype)

    z_ref[...] = jnp.dot(u_ref[...], w_ref[...].astype(BF16),
                         preferred_element_type=F32).astype(z_ref.dtype)
    w1b_ref[...] = w1_ref[...].astype(BF16)
    w2b_ref[...] = w2_ref[...].astype(BF16)
    wob_ref[...] = wo_ref[...].astype(BF16)


def _inproj(x2d, g, w_in, wa_p, w1, w2, wo, *, n_main, layer):
    t, d = x2d.shape
    dff = w1.shape[2]
    d_mix = wo.shape[1]
    nm = t // TM_IN
    nj = n_main // TN_IN
    steps = nm * nj
    r1, r2, ro = d // steps, dff // steps, d_mix // steps
    assert r1 * steps == d and r2 * steps == dff and ro * steps == d_mix
    assert r1 % 16 == 0 and r2 % 16 == 0 and ro % 16 == 0
    step = lambda i, j: i * nj + j
    vmem = (2 * TM_IN * d * 4 + TM_IN * d * 2 + 2 * d * TN_IN * 4 + d * TN_IN * 2
            + 2 * TM_IN * TN_IN * 2 + TM_IN * TN_IN * 4
            + 2 * 6 * (r1 * dff + r2 * d + ro * d) + 8 * MIB)
    return pl.pallas_call(
        _inproj_kernel,
        grid=(nm, nj),
        in_specs=[
            pl.BlockSpec((TM_IN, d), lambda i, j: (i, 0)),
            pl.BlockSpec((1, d), lambda i, j: (0, 0)),
            pl.BlockSpec((None, d, TN_IN), lambda i, j: (layer, 0, j)),
            pl.BlockSpec((d, LANES), lambda i, j: (0, 0)),
            pl.BlockSpec((None, r1, dff), lambda i, j: (layer, step(i, j), 0)),
            pl.BlockSpec((None, r2, d), lambda i, j: (layer, step(i, j), 0)),
            pl.BlockSpec((None, ro, d), lambda i, j: (layer, step(i, j), 0)),
        ],
        out_specs=[
            pl.BlockSpec((TM_IN, TN_IN), lambda i, j: (i, j)),
            pl.BlockSpec((TM_IN, LANES), lambda i, j: (i, 0)),
            pl.BlockSpec((r1, dff), lambda i, j: (step(i, j), 0)),
            pl.BlockSpec((r2, d), lambda i, j: (step(i, j), 0)),
            pl.BlockSpec((ro, d), lambda i, j: (step(i, j), 0)),
        ],
        out_shape=[
            jax.ShapeDtypeStruct((t, n_main), BF16),
            jax.ShapeDtypeStruct((t, LANES), BF16),
            jax.ShapeDtypeStruct((d, dff), BF16),
            jax.ShapeDtypeStruct((dff, d), BF16),
            jax.ShapeDtypeStruct((d_mix, d), BF16),
        ],
        scratch_shapes=[pltpu.VMEM((TM_IN, d), BF16)],
        compiler_params=pltpu.CompilerParams(
            dimension_semantics=("arbitrary", "arbitrary"), vmem_limit_bytes=vmem),
        name="inproj",
    )(x2d, g, w_in, wa_p, w1, w2, wo)


def _mixer_kernel(z_ref, za_ref, x_ref, wg_ref, bg_ref, cw_ref, cg_ref, gg_ref, wo_ref, us_ref,
                  o_ref, y_ref, yp_ref, st_ref, tail_ref, *, d_conv, d_k, d_v):
    tm = x_ref.shape[0]
    ti = pl.program_id(1)
    nc = tm // CHUNK
    dk = d_k // GLA_HEADS
    dv = d_v // GLA_HEADS
    oq = 3 * d_conv
    ok = oq + d_k
    ov = ok + d_k
    og = ov + d_v
    gw = d_conv // CONV_GROUPS

    @pl.when(ti == 0)
    def _():
        st_ref[...] = jnp.zeros_like(st_ref)
        tail_ref[...] = jnp.zeros_like(tail_ref)
        yp_ref[...] = jnp.zeros_like(yp_ref)

    w0 = cw_ref[0:1, :]
    w1 = cw_ref[1:2, :]
    w2 = cw_ref[2:3, :]
    prev = tail_ref[...]
    for c in range(nc):
        r = slice(c * CHUNK, (c + 1) * CHUNK)
        u = z_ref[r, d_conv:2 * d_conv].astype(F32) * z_ref[r, 2 * d_conv:3 * d_conv].astype(F32)
        ext = jnp.concatenate([prev, u], axis=0)
        u1 = pltpu.roll(ext, 1, axis=0)[SUBLANES:, :]
        u2 = pltpu.roll(ext, 2, axis=0)[SUBLANES:, :]
        conv = u2 * w0 + u1 * w1 + u * w2
        y = z_ref[r, 0:d_conv].astype(F32) * conv
        for g in range(CONV_GROUPS):
            cs = slice(g * gw, (g + 1) * gw)
            yg = y[:, cs]
            ms = jnp.mean(yg * yg, axis=-1, keepdims=True)
            y_ref[r, cs] = (yg * lax.rsqrt(ms + EPS) * cg_ref[:, cs]).astype(BF16)
        prev = u[CHUNK - SUBLANES:, :]
    tail_ref[...] = prev

    gate = jnp.dot(za_ref[...], wg_ref[...], preferred_element_type=F32) + bg_ref[...]
    la = (jnp.minimum(gate, 0.0) - jnp.log(1.0 + jnp.exp(-jnp.abs(gate)))) * (1.0 / GATE_NORMALIZER)
    la_hi = la.astype(BF16)
    la_lo = (la - la_hi.astype(F32)).astype(BF16)
    us = us_ref[...]
    suf = (jnp.dot(us, la_hi, preferred_element_type=F32)
           + jnp.dot(us, la_lo, preferred_element_type=F32))
    dec = jnp.exp(suf)

    kvts = []
    for c in range(nc):
        r = slice(c * CHUNK, (c + 1) * CHUNK)
        row = []
        for h in range(GLA_HEADS):
            ks = slice(h * dk, (h + 1) * dk)
            kd = (z_ref[r, ok + h * dk:ok + (h + 1) * dk].astype(F32) * dec[r, ks]).astype(BF16)
            vh = z_ref[r, ov + h * dv:ov + (h + 1) * dv]
            row.append(lax.dot_general(vh, kd, (((0,), (0,)), ((), ())),
                                       preferred_element_type=F32))
        kvts.append(row)

    def out_proj(part):
        cs = slice(part * (d_out // OUT_PARTS), (part + 1) * (d_out // OUT_PARTS))
        o_ref[:, cs] = x_ref[:, cs] + jnp.dot(yp_ref[...], wo_ref[:, cs],
                                              preferred_element_type=F32)

    d_out = o_ref.shape[1]
    out_proj(0)

    sts = []
    st = [st_ref[h] for h in range(GLA_HEADS)]
    for c in range(nc):
        b_end = suf[c * CHUNK:c * CHUNK + 1, :] + la[c * CHUNK:c * CHUNK + 1, :]
        a_end = jnp.exp(b_end)
        st = [st[h] * a_end[:, h * dk:(h + 1) * dk] + kvts[c][h] for h in range(GLA_HEADS)]
        sts.append([s.astype(BF16) for s in st])
    for h in range(GLA_HEADS):
        st_ref[h] = st[h]

    scale = dk ** -0.5
    gg = gg_ref[...]
    for c in range(nc):
        r = slice(c * CHUNK, (c + 1) * CHUNK)
        for h in range(GLA_HEADS):
            qs = (z_ref[r, oq + h * dk:oq + (h + 1) * dk].astype(F32) * scale).astype(BF16)
            o = lax.dot_general(qs, sts[c][h], (((1,), (1,)), ((), ())),
                                preferred_element_type=F32)
            ms = jnp.mean(o * o, axis=-1, keepdims=True)
            gt = z_ref[r, og + h * dv:og + (h + 1) * dv].astype(F32)
            silu = gt / (1.0 + jnp.exp(-gt))
            y_ref[r, d_conv + h * dv:d_conv + (h + 1) * dv] = (
                o * lax.rsqrt(ms + EPS) * gg * silu).astype(BF16)

    for part in range(1, OUT_PARTS):
        out_proj(part)
    yp_ref[...] = y_ref[...]


def _mixer(z3, za3, x, wg_p, bg, cw, cg, gg, wo, us, *, d_conv, d_k, d_v):
    b, s, d = x.shape
    nz = z3.shape[-1]
    d_mix = wo.shape[0]
    dk = d_k // GLA_HEADS
    dv = d_v // GLA_HEADS
    const = lambda shape: pl.BlockSpec(shape, lambda bi, ti: (0,) * len(shape))
    vmem = (2 * TM_MIX * nz * 2 + 4 * TM_MIX * d * 4 + 2 * d_mix * d * 2 + 2 * TM_MIX * d_mix * 2
            + GLA_HEADS * dv * dk * 4 + 16 * MIB)
    kern = functools.partial(_mixer_kernel, d_conv=d_conv, d_k=d_k, d_v=d_v)
    nt = s // TM_MIX
    cur = lambda bi, ti: (bi, jnp.minimum(ti, nt - 1), 0)
    prv = lambda bi, ti: (bi, jnp.maximum(ti - 1, 0), 0)
    return pl.pallas_call(
        kern,
        grid=(b, nt + 1),
        in_specs=[
            pl.BlockSpec((None, TM_MIX, nz), cur),
            pl.BlockSpec((None, TM_MIX, LANES), cur),
            pl.BlockSpec((None, TM_MIX, d), prv),
            const(wg_p.shape), const(bg.shape), const(cw.shape), const(cg.shape), const(gg.shape),
            const(wo.shape), const(us.shape),
        ],
        out_specs=pl.BlockSpec((None, TM_MIX, d), prv),
        out_shape=jax.ShapeDtypeStruct((b, s, d), F32),
        scratch_shapes=[
            pltpu.VMEM((TM_MIX, d_mix), BF16),
            pltpu.VMEM((TM_MIX, d_mix), BF16),
            pltpu.VMEM((GLA_HEADS, dv, dk), F32),
            pltpu.VMEM((SUBLANES, d_conv), F32),
        ],
        compiler_params=pltpu.CompilerParams(
            dimension_semantics=("arbitrary", "arbitrary"), vmem_limit_bytes=vmem),
        name="mixer",
    )(z3, za3, x, wg_p, bg, cw, cg, gg, wo, us)


def _mlp_kernel(x_ref, g2_ref, w1_ref, w2_ref, gf_ref, o_ref, h_ref):
    f = pl.program_id(1)
    nrow = x_ref.shape[0] // ROWS_NORM

    @pl.when(f == 0)
    def _():
        g = g2_ref[...]

        def body(i, c):
            r = pl.ds(pl.multiple_of(i * ROWS_NORM, ROWS_NORM), ROWS_NORM)
            x = x_ref[r, :]
            h_ref[r, :] = _rms_rows(x, g).astype(BF16)
            o_ref[r, :] = x
            return c

        lax.fori_loop(0, nrow, body, 0)

    a = jnp.dot(h_ref[...], w1_ref[...], preferred_element_type=F32)
    a = jnp.maximum(a, 0.0)
    o_ref[...] += jnp.dot((a * a).astype(BF16), w2_ref[...], preferred_element_type=F32)

    @pl.when(f == pl.num_programs(1) - 1)
    def _():
        g = gf_ref[...]

        def body(i, c):
            r = pl.ds(pl.multiple_of(i * ROWS_NORM, ROWS_NORM), ROWS_NORM)
            o_ref[r, :] = _rms_rows(o_ref[r, :], g)
            return c

        lax.fori_loop(0, nrow, body, 0)


def _mlp(x2d, g2, w1, w2, gf):
    t, d = x2d.shape
    dff = w1.shape[1]
    vmem = (4 * TM_MLP * d * 4 + TM_MLP * d * 2 + 4 * d * TF_MLP * 2
            + TM_MLP * TF_MLP * 6 + 8 * MIB)
    return pl.pallas_call(
        _mlp_kernel,
        grid=(t // TM_MLP, dff // TF_MLP),
        in_specs=[
            pl.BlockSpec((TM_MLP, d), lambda i, f: (i, 0)),
            pl.BlockSpec((1, d), lambda i, f: (0, 0)),
            pl.BlockSpec((d, TF_MLP), lambda i, f: (0, f)),
            pl.BlockSpec((TF_MLP, d), lambda i, f: (f, 0)),
            pl.BlockSpec((1, d), lambda i, f: (0, 0)),
        ],
        out_specs=pl.BlockSpec((TM_MLP, d), lambda i, f: (i, 0)),
        out_shape=jax.ShapeDtypeStruct((t, d), F32),
        scratch_shapes=[pltpu.VMEM((TM_MLP, d), BF16)],
        compiler_params=pltpu.CompilerParams(
            dimension_semantics=("arbitrary", "arbitrary"), vmem_limit_bytes=vmem),
        name="mlp",
    )(x2d, g2, w1, w2, gf)


def _suffix_mask(tm):
    t = np.arange(tm)
    m = (t[:, None] // CHUNK == t[None, :] // CHUNK) & (t[None, :] > t[:, None])
    return jnp.asarray(m, dtype=BF16)


def kernel(x, norm1_g, w_in, w_gate_up, b_gate, conv_w, conv_norm_g, gla_norm_g, w_out, norm2_g,
           w_ff1, w_ff2, norm_f_g):
    b, s, d = x.shape
    depth = w_in.shape[0]
    d_conv = conv_w.shape[1]
    d_k = w_gate_up.shape[2]
    d_v = gla_norm_g.shape[1] * GLA_HEADS
    n_main = 3 * d_conv + 2 * d_k + 2 * d_v
    assert w_in.shape[2] == n_main + GATE_RANK
    assert depth == 1, "the final norm is fused into the (single) layer's MLP call"
    us = _suffix_mask(TM_MIX)
    for l in range(depth):
        wa_p = jnp.pad(w_in[l, :, n_main:].astype(BF16), ((0, 0), (0, LANES - GATE_RANK)))
        wg_p = jnp.pad(w_gate_up[l].astype(BF16), ((0, LANES - GATE_RANK), (0, 0)))
        z, za, w1b, w2b, wob = _inproj(x.reshape(b * s, d), norm1_g[l][None, :], w_in, wa_p,
                                       w_ff1, w_ff2, w_out, n_main=n_main, layer=l)
        x = _mixer(z.reshape(b, s, -1), za.reshape(b, s, -1), x, wg_p, b_gate[l][None, :],
                   conv_w[l].T, conv_norm_g[l][None, :], gla_norm_g[l][None, :], wob, us,
                   d_conv=d_conv, d_k=d_k, d_v=d_v)
        x = _mlp(x.reshape(b * s, d), norm2_g[l][None, :], w1b, w2b,
                 norm_f_g[None, :]).reshape(b, s, d)
    return x
```

```python
import functools

import numpy as np
import jax
import jax.numpy as jnp
from jax import lax
from jax.experimental import pallas as pl
from jax.experimental.pallas import tpu as pltpu

F32 = jnp.float32
BF16 = jnp.bfloat16

EPS = 1e-6
CHUNK = 64
CONV_GROUPS = 8
CONV_WIDTH = 3
GLA_HEADS = 4
GATE_RANK = 16
GATE_NORMALIZER = 16.0

LANES = 128
SUBLANES = 8
MXU_COLS = 256
MIB = 1024 * 1024

TM_IN = 1024
TN_IN = 1024
CAST_STEPS = 64
TM_MIX = 512
TM_MLP = 1024
TF_MLP = 512
ROWS_NORM = 64
OUT_PARTS = 4


def _rms_rows(x, g):
    ms = jnp.mean(x * x, axis=-1, keepdims=True)
    return x * lax.rsqrt(ms + EPS) * g


def _inproj_kernel(x_ref, g_ref, w_ref, wa_ref, w1_ref, w2_ref, wo_ref,
                   z_ref, za_ref, w1b_ref, w2b_ref, wob_ref, u_ref):
    nt = (((1,), (1,)), ((), ()))

    @pl.when(pl.program_id(1) == 0)
    def _():
        g = g_ref[...]

        def body(i, c):
            r = pl.ds(pl.multiple_of(i * ROWS_NORM, ROWS_NORM), ROWS_NORM)
            u_ref[r, :] = _rms_rows(x_ref[r, :], g).astype(BF16)
            return c

        lax.fori_loop(0, x_ref.shape[0] // ROWS_NORM, body, 0)
        za_ref[...] = lax.dot_general(u_ref[...], wa_ref[...], nt,
                                      preferred_element_type=F32).astype(za_ref.dtype)

    z_ref[...] = lax.dot_general(u_ref[...], w_ref[...], nt,
                                 preferred_element_type=F32).astype(z_ref.dtype)
    w1b_ref[...] = w1_ref[...].astype(BF16)
    w2b_ref[...] = w2_ref[...].astype(BF16)
    wob_ref[...] = wo_ref[...].astype(BF16)


def _inproj(x2d, g, wt, wa_p, w1, w2, wo, *, n_main, layer):
    t, d = x2d.shape
    dff = w1.shape[2]
    d_mix = wo.shape[1]
    nm = t // TM_IN
    nj = n_main // TN_IN
    assert nm * nj >= CAST_STEPS
    r1, r2, ro = d // CAST_STEPS, dff // CAST_STEPS, d_mix // CAST_STEPS
    assert r1 % 16 == 0 and r2 % 16 == 0 and ro % 16 == 0
    step = lambda i, j: jnp.minimum(i * nj + j, CAST_STEPS - 1)
    vmem = (2 * TM_IN * d * 4 + TM_IN * d * 2 + 2 * d * TN_IN * 2
            + 2 * TM_IN * TN_IN * 2 + TM_IN * TN_IN * 4
            + 2 * 6 * (r1 * dff + r2 * d + ro * d) + 8 * MIB)
    return pl.pallas_call(
        _inproj_kernel,
        grid=(nm, nj),
        in_specs=[
            pl.BlockSpec((TM_IN, d), lambda i, j: (i, 0)),
            pl.BlockSpec((1, d), lambda i, j: (0, 0)),
            pl.BlockSpec((None, TN_IN, d), lambda i, j: (layer, j, 0)),
            pl.BlockSpec((LANES, d), lambda i, j: (0, 0)),
            pl.BlockSpec((None, r1, dff), lambda i, j: (layer, step(i, j), 0)),
            pl.BlockSpec((None, r2, d), lambda i, j: (layer, step(i, j), 0)),
            pl.BlockSpec((None, ro, d), lambda i, j: (layer, step(i, j), 0)),
        ],
        out_specs=[
            pl.BlockSpec((TM_IN, TN_IN), lambda i, j: (i, j)),
            pl.BlockSpec((TM_IN, LANES), lambda i, j: (i, 0)),
            pl.BlockSpec((r1, dff), lambda i, j: (step(i, j), 0)),
            pl.BlockSpec((r2, d), lambda i, j: (step(i, j), 0)),
            pl.BlockSpec((ro, d), lambda i, j: (step(i, j), 0)),
        ],
        out_shape=[
            jax.ShapeDtypeStruct((t, n_main), BF16),
            jax.ShapeDtypeStruct((t, LANES), BF16),
            jax.ShapeDtypeStruct((d, dff), BF16),
            jax.ShapeDtypeStruct((dff, d), BF16),
            jax.ShapeDtypeStruct((d_mix, d), BF16),
        ],
        scratch_shapes=[pltpu.VMEM((TM_IN, d), BF16)],
        compiler_params=pltpu.CompilerParams(
            dimension_semantics=("arbitrary", "arbitrary"), vmem_limit_bytes=vmem),
        name="inproj",
    )(x2d, g, wt, wa_p, w1, w2, wo)


def _mixer_kernel(z_ref, za_ref, x_ref, wg_ref, bg_ref, cw_ref, cg_ref, gg_ref, wo_ref, us_ref,
                  o_ref, y_ref, yp_ref, st_ref, tail_ref, *, d_conv, d_k, d_v):
    tm = x_ref.shape[0]
    ti = pl.program_id(1)
    nc = tm // CHUNK
    dk = d_k // GLA_HEADS
    dv = d_v // GLA_HEADS
    oq = 3 * d_conv
    ok = oq + d_k
    ov = ok + d_k
    og = ov + d_v
    gw = d_conv // CONV_GROUPS

    @pl.when(ti == 0)
    def _():
        st_ref[...] = jnp.zeros_like(st_ref)
        tail_ref[...] = jnp.zeros_like(tail_ref)
        yp_ref[...] = jnp.zeros_like(yp_ref)

    w0 = cw_ref[0:1, :]
    w1 = cw_ref[1:2, :]
    w2 = cw_ref[2:3, :]
    prev = tail_ref[...]
    for c in range(nc):
        r = slice(c * CHUNK, (c + 1) * CHUNK)
        u = z_ref[r, d_conv:2 * d_conv].astype(F32) * z_ref[r, 2 * d_conv:3 * d_conv].astype(F32)
        ext = jnp.concatenate([prev, u], axis=0)
        u1 = pltpu.roll(ext, 1, axis=0)[SUBLANES:, :]
        u2 = pltpu.roll(ext, 2, axis=0)[SUBLANES:, :]
        conv = u2 * w0 + u1 * w1 + u * w2
        y = z_ref[r, 0:d_conv].astype(F32) * conv
        for g in range(CONV_GROUPS):
            cs = slice(g * gw, (g + 1) * gw)
            yg = y[:, cs]
            ms = jnp.mean(yg * yg, axis=-1, keepdims=True)
            y_ref[r, cs] = (yg * lax.rsqrt(ms + EPS) * cg_ref[:, cs]).astype(BF16)
        prev = u[CHUNK - SUBLANES:, :]
    tail_ref[...] = prev

    gate = jnp.dot(za_ref[...], wg_ref[...], preferred_element_type=F32) + bg_ref[...]
    la = (jnp.minimum(gate, 0.0) - jnp.log(1.0 + jnp.exp(-jnp.abs(gate)))) * (1.0 / GATE_NORMALIZER)
    la_hi = la.astype(BF16)
    la_lo = (la - la_hi.astype(F32)).astype(BF16)
    us = us_ref[...]
    suf = (jnp.dot(us, la_hi, preferred_element_type=F32)
           + jnp.dot(us, la_lo, preferred_element_type=F32))
    dec = jnp.exp(suf)

    kvts = []
    for c in range(nc):
        r = slice(c * CHUNK, (c + 1) * CHUNK)
        row = []
        for h in range(GLA_HEADS):
            ks = slice(h * dk, (h + 1) * dk)
            kd = (z_ref[r, ok + h * dk:ok + (h + 1) * dk].astype(F32) * dec[r, ks]).astype(BF16)
            vh = z_ref[r, ov + h * dv:ov + (h + 1) * dv]
            row.append(lax.dot_general(vh, kd, (((0,), (0,)), ((), ())),
                                       preferred_element_type=F32))
        kvts.append(row)

    def out_proj(part):
        cs = slice(part * (d_out // OUT_PARTS), (part + 1) * (d_out // OUT_PARTS))
        o_ref[:, cs] = x_ref[:, cs] + jnp.dot(yp_ref[...], wo_ref[:, cs],
                                              preferred_element_type=F32)

    d_out = o_ref.shape[1]
    out_proj(0)

    sts = []
    st = [st_ref[h] for h in range(GLA_HEADS)]
    for c in range(nc):
        b_end = suf[c * CHUNK:c * CHUNK + 1, :] + la[c * CHUNK:c * CHUNK + 1, :]
        a_end = jnp.exp(b_end)
        st = [st[h] * a_end[:, h * dk:(h + 1) * dk] + kvts[c][h] for h in range(GLA_HEADS)]
        sts.append([s.astype(BF16) for s in st])
    for h in range(GLA_HEADS):
        st_ref[h] = st[h]

    scale = dk ** -0.5
    gg = gg_ref[...]
    for c in range(nc):
        r = slice(c * CHUNK, (c + 1) * CHUNK)
        for h in range(GLA_HEADS):
            qs = (z_ref[r, oq + h * dk:oq + (h + 1) * dk].astype(F32) * scale).astype(BF16)
            o = lax.dot_general(qs, sts[c][h], (((1,), (1,)), ((), ())),
                                preferred_element_type=F32)
            ms = jnp.mean(o * o, axis=-1, keepdims=True)
            gt = z_ref[r, og + h * dv:og + (h + 1) * dv].astype(F32)
            silu = gt / (1.0 + jnp.exp(-gt))
            y_ref[r, d_conv + h * dv:d_conv + (h + 1) * dv] = (
                o * lax.rsqrt(ms + EPS) * gg * silu).astype(BF16)

    for part in range(1, OUT_PARTS):
        out_proj(part)
    yp_ref[...] = y_ref[...]


def _mixer(z3, za3, x, wg_p, bg, cw, cg, gg, wo, us, *, d_conv, d_k, d_v):
    b, s, d = x.shape
    nz = z3.shape[-1]
    d_mix = wo.shape[0]
    dk = d_k // GLA_HEADS
    dv = d_v // GLA_HEADS
    const = lambda shape: pl.BlockSpec(shape, lambda bi, ti: (0,) * len(shape))
    vmem = (2 * TM_MIX * nz * 2 + 4 * TM_MIX * d * 4 + 2 * d_mix * d * 2 + 2 * TM_MIX * d_mix * 2
            + GLA_HEADS * dv * dk * 4 + 16 * MIB)
    kern = functools.partial(_mixer_kernel, d_conv=d_conv, d_k=d_k, d_v=d_v)
    nt = s // TM_MIX
    cur = lambda bi, ti: (bi, jnp.minimum(ti, nt - 1), 0)
    prv = lambda bi, ti: (bi, jnp.maximum(ti - 1, 0), 0)
    return pl.pallas_call(
        kern,
        grid=(b, nt + 1),
        in_specs=[
            pl.BlockSpec((None, TM_MIX, nz), cur),
            pl.BlockSpec((None, TM_MIX, LANES), cur),
            pl.BlockSpec((None, TM_MIX, d), prv),
            const(wg_p.shape), const(bg.shape), const(cw.shape), const(cg.shape), const(gg.shape),
            const(wo.shape), const(us.shape),
        ],
        out_specs=pl.BlockSpec((None, TM_MIX, d), prv),
        out_shape=jax.ShapeDtypeStruct((b, s, d), F32),
        scratch_shapes=[
            pltpu.VMEM((TM_MIX, d_mix), BF16),
            pltpu.VMEM((TM_MIX, d_mix), BF16),
            pltpu.VMEM((GLA_HEADS, dv, dk), F32),
            pltpu.VMEM((SUBLANES, d_conv), F32),
        ],
        compiler_params=pltpu.CompilerParams(
            dimension_semantics=("arbitrary", "arbitrary"), vmem_limit_bytes=vmem),
        name="mixer",
    )(z3, za3, x, wg_p, bg, cw, cg, gg, wo, us)


def _mlp_kernel(x_ref, g2_ref, w1_ref, w2_ref, gf_ref, o_ref, h_ref):
    f = pl.program_id(1)
    nrow = x_ref.shape[0] // ROWS_NORM

    @pl.when(f == 0)
    def _():
        g = g2_ref[...]

        def body(i, c):
            r = pl.ds(pl.multiple_of(i * ROWS_NORM, ROWS_NORM), ROWS_NORM)
            x = x_ref[r, :]
            h_ref[r, :] = _rms_rows(x, g).astype(BF16)
            o_ref[r, :] = x
            return c

        lax.fori_loop(0, nrow, body, 0)

    a = jnp.dot(h_ref[...], w1_ref[...], preferred_element_type=F32)
    a = jnp.maximum(a, 0.0)
    o_ref[...] += jnp.dot((a * a).astype(BF16), w2_ref[...], preferred_element_type=F32)

    @pl.when(f == pl.num_programs(1) - 1)
    def _():
        g = gf_ref[...]

        def body(i, c):
            r = pl.ds(pl.multiple_of(i * ROWS_NORM, ROWS_NORM), ROWS_NORM)
            o_ref[r, :] = _rms_rows(o_ref[r, :], g)
            return c

        lax.fori_loop(0, nrow, body, 0)


def _mlp(x2d, g2, w1, w2, gf):
    t, d = x2d.shape
    dff = w1.shape[1]
    vmem = (4 * TM_MLP * d * 4 + TM_MLP * d * 2 + 4 * d * TF_MLP * 2
            + TM_MLP * TF_MLP * 6 + 8 * MIB)
    return pl.pallas_call(
        _mlp_kernel,
        grid=(t // TM_MLP, dff // TF_MLP),
        in_specs=[
            pl.BlockSpec((TM_MLP, d), lambda i, f: (i, 0)),
            pl.BlockSpec((1, d), lambda i, f: (0, 0)),
            pl.BlockSpec((d, TF_MLP), lambda i, f: (0, f)),
            pl.BlockSpec((TF_MLP, d), lambda i, f: (f, 0)),
            pl.BlockSpec((1, d), lambda i, f: (0, 0)),
        ],
        out_specs=pl.BlockSpec((TM_MLP, d), lambda i, f: (i, 0)),
        out_shape=jax.ShapeDtypeStruct((t, d), F32),
        scratch_shapes=[pltpu.VMEM((TM_MLP, d), BF16)],
        compiler_params=pltpu.CompilerParams(
            dimension_semantics=("arbitrary", "arbitrary"), vmem_limit_bytes=vmem),
        name="mlp",
    )(x2d, g2, w1, w2, gf)


def _suffix_mask(tm):
    t = np.arange(tm)
    m = (t[:, None] // CHUNK == t[None, :] // CHUNK) & (t[None, :] > t[:, None])
    return jnp.asarray(m, dtype=BF16)


def kernel(x, norm1_g, w_in, w_gate_up, b_gate, conv_w, conv_norm_g, gla_norm_g, w_out, norm2_g,
           w_ff1, w_ff2, norm_f_g):
    b, s, d = x.shape
    depth = w_in.shape[0]
    d_conv = conv_w.shape[1]
    d_k = w_gate_up.shape[2]
    d_v = gla_norm_g.shape[1] * GLA_HEADS
    n_main = 3 * d_conv + 2 * d_k + 2 * d_v
    assert w_in.shape[2] == n_main + GATE_RANK
    assert depth == 1, "the final norm is fused into the (single) layer's MLP call"
    us = _suffix_mask(TM_MIX)
    for l in range(depth):
        wt = jnp.swapaxes(w_in, 1, 2).astype(BF16)
        wa_p = jnp.pad(wt[l, n_main:, :], ((0, LANES - GATE_RANK), (0, 0)))
        wg_p = jnp.pad(w_gate_up[l].astype(BF16), ((0, LANES - GATE_RANK), (0, 0)))
        z, za, w1b, w2b, wob = _inproj(x.reshape(b * s, d), norm1_g[l][None, :], wt, wa_p,
                                       w_ff1, w_ff2, w_out, n_main=n_main, layer=l)
        x = _mixer(z.reshape(b, s, -1), za.reshape(b, s, -1), x, wg_p, b_gate[l][None, :],
                   conv_w[l].T, conv_norm_g[l][None, :], gla_norm_g[l][None, :], wob, us,
                   d_conv=d_conv, d_k=d_k, d_v=d_v)
        x = _mlp(x.reshape(b * s, d), norm2_g[l][None, :], w1b, w2b,
                 norm_f_g[None, :]).reshape(b, s, d)
    return x
```

```python
import functools

import numpy as np
import jax
import jax.numpy as jnp
from jax import lax
from jax.experimental import pallas as pl
from jax.experimental.pallas import tpu as pltpu

F32 = jnp.float32
BF16 = jnp.bfloat16

EPS = 1e-6
CHUNK = 64
CONV_GROUPS = 8
CONV_WIDTH = 3
GLA_HEADS = 4
GATE_RANK = 16
GATE_NORMALIZER = 16.0

LANES = 128
SUBLANES = 8
MIB = 1024 * 1024

TM_IN = 1024
TN_IN = 1024
CAST_STEPS = 64
TM_MIX = 512
TM_MLP = 1024
TF_MLP = 1024
ROWS_NORM = 256
OUT_PARTS = 4


def _rms_rows(x, g):
    ms = jnp.mean(x * x, axis=-1, keepdims=True)
    return x * lax.rsqrt(ms + EPS) * g


def _inproj_kernel(x_ref, g_ref, w_ref, wa_ref, w1_ref, w2_ref, wo_ref,
                   z_ref, za_ref, w1b_ref, w2b_ref, wob_ref, u_ref):
    nt = (((1,), (1,)), ((), ()))

    @pl.when(pl.program_id(1) == 0)
    def _():
        g = g_ref[...]

        def body(i, c):
            r = pl.ds(pl.multiple_of(i * ROWS_NORM, ROWS_NORM), ROWS_NORM)
            u_ref[r, :] = _rms_rows(x_ref[r, :], g).astype(BF16)
            return c

        lax.fori_loop(0, x_ref.shape[0] // ROWS_NORM, body, 0)
        za_ref[...] = lax.dot_general(u_ref[...], wa_ref[...], nt,
                                      preferred_element_type=F32).astype(za_ref.dtype)

    z_ref[...] = lax.dot_general(u_ref[...], w_ref[...], nt,
                                 preferred_element_type=F32).astype(z_ref.dtype)
    w1b_ref[...] = w1_ref[...].astype(BF16)
    w2b_ref[...] = w2_ref[...].astype(BF16)
    wob_ref[...] = wo_ref[...].astype(BF16)


def _inproj(x2d, g, wt, wa_p, w1, w2, wo, *, n_main, layer):
    t, d = x2d.shape
    dff = w1.shape[2]
    d_mix = wo.shape[1]
    nm = t // TM_IN
    nj = n_main // TN_IN
    assert nm * nj >= CAST_STEPS
    r1, r2, ro = d // CAST_STEPS, dff // CAST_STEPS, d_mix // CAST_STEPS
    assert r1 % 16 == 0 and r2 % 16 == 0 and ro % 16 == 0
    step = lambda i, j: jnp.minimum(i * nj + j, CAST_STEPS - 1)
    vmem = (2 * TM_IN * d * 4 + TM_IN * d * 2 + 2 * d * TN_IN * 2
            + 2 * TM_IN * TN_IN * 2 + TM_IN * TN_IN * 4
            + 2 * 6 * (r1 * dff + r2 * d + ro * d) + 8 * MIB)
    return pl.pallas_call(
        _inproj_kernel,
        grid=(nm, nj),
        in_specs=[
            pl.BlockSpec((TM_IN, d), lambda i, j: (i, 0)),
            pl.BlockSpec((1, d), lambda i, j: (0, 0)),
            pl.BlockSpec((None, TN_IN, d), lambda i, j: (layer, j, 0)),
            pl.BlockSpec((LANES, d), lambda i, j: (0, 0)),
            pl.BlockSpec((None, r1, dff), lambda i, j: (layer, step(i, j), 0)),
            pl.BlockSpec((None, r2, d), lambda i, j: (layer, step(i, j), 0)),
            pl.BlockSpec((None, ro, d), lambda i, j: (layer, step(i, j), 0)),
        ],
        out_specs=[
            pl.BlockSpec((TM_IN, TN_IN), lambda i, j: (i, j)),
            pl.BlockSpec((TM_IN, LANES), lambda i, j: (i, 0)),
            pl.BlockSpec((r1, dff), lambda i, j: (step(i, j), 0)),
            pl.BlockSpec((r2, d), lambda i, j: (step(i, j), 0)),
            pl.BlockSpec((ro, d), lambda i, j: (step(i, j), 0)),
        ],
        out_shape=[
            jax.ShapeDtypeStruct((t, n_main), BF16),
            jax.ShapeDtypeStruct((t, LANES), BF16),
            jax.ShapeDtypeStruct((d, dff), BF16),
            jax.ShapeDtypeStruct((dff, d), BF16),
            jax.ShapeDtypeStruct((d_mix, d), BF16),
        ],
        scratch_shapes=[pltpu.VMEM((TM_IN, d), BF16)],
        compiler_params=pltpu.CompilerParams(
            dimension_semantics=("arbitrary", "arbitrary"), vmem_limit_bytes=vmem),
        name="inproj",
    )(x2d, g, wt, wa_p, w1, w2, wo)


def _mixer_kernel(z_ref, za_ref, x_ref, wg_ref, bg_ref, cw_ref, cg_ref, gg_ref, wo_ref, us_ref,
                  o_ref, y_ref, yp_ref, st_ref, tail_ref, *, d_conv, d_k, d_v):
    tm = x_ref.shape[0]
    ti = pl.program_id(1)
    nc = tm // CHUNK
    dk = d_k // GLA_HEADS
    dv = d_v // GLA_HEADS
    oq = 3 * d_conv
    ok = oq + d_k
    ov = ok + d_k
    og = ov + d_v
    gw = d_conv // CONV_GROUPS

    @pl.when(ti == 0)
    def _():
        st_ref[...] = jnp.zeros_like(st_ref)
        tail_ref[...] = jnp.zeros_like(tail_ref)
        yp_ref[...] = jnp.zeros_like(yp_ref)

    w0 = cw_ref[0:1, :]
    w1 = cw_ref[1:2, :]
    w2 = cw_ref[2:3, :]
    prev = tail_ref[...]
    for c in range(nc):
        r = slice(c * CHUNK, (c + 1) * CHUNK)
        u = z_ref[r, d_conv:2 * d_conv].astype(F32) * z_ref[r, 2 * d_conv:3 * d_conv].astype(F32)
        ext = jnp.concatenate([prev, u], axis=0)
        u1 = pltpu.roll(ext, 1, axis=0)[SUBLANES:, :]
        u2 = pltpu.roll(ext, 2, axis=0)[SUBLANES:, :]
        conv = u2 * w0 + u1 * w1 + u * w2
        y = z_ref[r, 0:d_conv].astype(F32) * conv
        for g in range(CONV_GROUPS):
            cs = slice(g * gw, (g + 1) * gw)
            yg = y[:, cs]
            ms = jnp.mean(yg * yg, axis=-1, keepdims=True)
            y_ref[r, cs] = (yg * lax.rsqrt(ms + EPS) * cg_ref[:, cs]).astype(BF16)
        prev = u[CHUNK - SUBLANES:, :]
    tail_ref[...] = prev

    gate = jnp.dot(za_ref[...], wg_ref[...], preferred_element_type=F32) + bg_ref[...]
    la = (jnp.minimum(gate, 0.0) - jnp.log(1.0 + jnp.exp(-jnp.abs(gate)))) * (1.0 / GATE_NORMALIZER)
    la_hi = la.astype(BF16)
    la_lo = (la - la_hi.astype(F32)).astype(BF16)
    us = us_ref[...]
    suf = (jnp.dot(us, la_hi, preferred_element_type=F32)
           + jnp.dot(us, la_lo, preferred_element_type=F32))
    dec = jnp.exp(suf)

    kvts = []
    for c in range(nc):
        r = slice(c * CHUNK, (c + 1) * CHUNK)
        row = []
        for h in range(GLA_HEADS):
            ks = slice(h * dk, (h + 1) * dk)
            kd = (z_ref[r, ok + h * dk:ok + (h + 1) * dk].astype(F32) * dec[r, ks]).astype(BF16)
            vh = z_ref[r, ov + h * dv:ov + (h + 1) * dv]
            row.append(lax.dot_general(vh, kd, (((0,), (0,)), ((), ())),
                                       preferred_element_type=F32))
        kvts.append(row)

    def out_proj(part):
        cs = slice(part * (d_out // OUT_PARTS), (part + 1) * (d_out // OUT_PARTS))
        o_ref[:, cs] = x_ref[:, cs] + jnp.dot(yp_ref[...], wo_ref[:, cs],
                                              preferred_element_type=F32)

    d_out = o_ref.shape[1]
    out_proj(0)

    sts = []
    st = [st_ref[h] for h in range(GLA_HEADS)]
    for c in range(nc):
        b_end = suf[c * CHUNK:c * CHUNK + 1, :] + la[c * CHUNK:c * CHUNK + 1, :]
        a_end = jnp.exp(b_end)
        st = [st[h] * a_end[:, h * dk:(h + 1) * dk] + kvts[c][h] for h in range(GLA_HEADS)]
        sts.append([s.astype(BF16) for s in st])
    for h in range(GLA_HEADS):
        st_ref[h] = st[h]

    scale = dk ** -0.5
    gg = gg_ref[...]
    for c in range(nc):
        r = slice(c * CHUNK, (c + 1) * CHUNK)
        for h in range(GLA_HEADS):
            qs = (z_ref[r, oq + h * dk:oq + (h + 1) * dk].astype(F32) * scale).astype(BF16)
            o = lax.dot_general(qs, sts[c][h], (((1,), (1,)), ((), ())),
                                preferred_element_type=F32)
            ms = jnp.mean(o * o, axis=-1, keepdims=True)
            gt = z_ref[r, og + h * dv:og + (h + 1) * dv].astype(F32)
            silu = gt / (1.0 + jnp.exp(-gt))
            y_ref[r, d_conv + h * dv:d_conv + (h + 1) * dv] = (
                o * lax.rsqrt(ms + EPS) * gg * silu).astype(BF16)

    for part in range(1, OUT_PARTS):
        out_proj(part)
    yp_ref[...] = y_ref[...]


def _mixer(z3, za3, x, wg_p, bg, cw, cg, gg, wo, us, *, d_conv, d_k, d_v):
    b, s, d = x.shape
    nz = z3.shape[-1]
    d_mix = wo.shape[0]
    dk = d_k // GLA_HEADS
    dv = d_v // GLA_HEADS
    const = lambda shape: pl.BlockSpec(shape, lambda bi, ti: (0,) * len(shape))
    vmem = (2 * TM_MIX * nz * 2 + 4 * TM_MIX * d * 4 + 2 * d_mix * d * 2 + 2 * TM_MIX * d_mix * 2
            + GLA_HEADS * dv * dk * 4 + 16 * MIB)
    kern = functools.partial(_mixer_kernel, d_conv=d_conv, d_k=d_k, d_v=d_v)
    nt = s // TM_MIX
    cur = lambda bi, ti: (bi, jnp.minimum(ti, nt - 1), 0)
    prv = lambda bi, ti: (bi, jnp.maximum(ti - 1, 0), 0)
    return pl.pallas_call(
        kern,
        grid=(b, nt + 1),
        in_specs=[
            pl.BlockSpec((None, TM_MIX, nz), cur),
            pl.BlockSpec((None, TM_MIX, LANES), cur),
            pl.BlockSpec((None, TM_MIX, d), prv),
            const(wg_p.shape), const(bg.shape), const(cw.shape), const(cg.shape), const(gg.shape),
            const(wo.shape), const(us.shape),
        ],
        out_specs=pl.BlockSpec((None, TM_MIX, d), prv),
        out_shape=jax.ShapeDtypeStruct((b, s, d), F32),
        scratch_shapes=[
            pltpu.VMEM((TM_MIX, d_mix), BF16),
            pltpu.VMEM((TM_MIX, d_mix), BF16),
            pltpu.VMEM((GLA_HEADS, dv, dk), F32),
            pltpu.VMEM((SUBLANES, d_conv), F32),
        ],
        compiler_params=pltpu.CompilerParams(
            dimension_semantics=("arbitrary", "arbitrary"), vmem_limit_bytes=vmem),
        name="mixer",
    )(z3, za3, x, wg_p, bg, cw, cg, gg, wo, us)


def _mlp_kernel(x_ref, g2_ref, w1_ref, w2_ref, gf_ref, o_ref, h_ref):
    f = pl.program_id(1)
    nrow = x_ref.shape[0] // ROWS_NORM

    @pl.when(f == 0)
    def _():
        g = g2_ref[...]

        def body(i, c):
            r = pl.ds(pl.multiple_of(i * ROWS_NORM, ROWS_NORM), ROWS_NORM)
            x = x_ref[r, :]
            h_ref[r, :] = _rms_rows(x, g).astype(BF16)
            o_ref[r, :] = x
            return c

        lax.fori_loop(0, nrow, body, 0)

    a = jnp.dot(h_ref[...], w1_ref[...], preferred_element_type=F32)
    a = jnp.maximum(a, 0.0)
    o_ref[...] += jnp.dot((a * a).astype(BF16), w2_ref[...], preferred_element_type=F32)

    @pl.when(f == pl.num_programs(1) - 1)
    def _():
        g = gf_ref[...]

        def body(i, c):
            r = pl.ds(pl.multiple_of(i * ROWS_NORM, ROWS_NORM), ROWS_NORM)
            o_ref[r, :] = _rms_rows(o_ref[r, :], g)
            return c

        lax.fori_loop(0, nrow, body, 0)


def _mlp(x2d, g2, w1, w2, gf):
    t, d = x2d.shape
    dff = w1.shape[1]
    vmem = (4 * TM_MLP * d * 4 + TM_MLP * d * 2 + 4 * d * TF_MLP * 2
            + TM_MLP * TF_MLP * 6 + 2 * MIB)
    return pl.pallas_call(
        _mlp_kernel,
        grid=(t // TM_MLP, dff // TF_MLP),
        in_specs=[
            pl.BlockSpec((TM_MLP, d), lambda i, f: (i, 0)),
            pl.BlockSpec((1, d), lambda i, f: (0, 0)),
            pl.BlockSpec((d, TF_MLP), lambda i, f: (0, f)),
            pl.BlockSpec((TF_MLP, d), lambda i, f: (f, 0)),
            pl.BlockSpec((1, d), lambda i, f: (0, 0)),
        ],
        out_specs=pl.BlockSpec((TM_MLP, d), lambda i, f: (i, 0)),
        out_shape=jax.ShapeDtypeStruct((t, d), F32),
        scratch_shapes=[pltpu.VMEM((TM_MLP, d), BF16)],
        compiler_params=pltpu.CompilerParams(
            dimension_semantics=("arbitrary", "arbitrary"), vmem_limit_bytes=vmem),
        name="mlp",
    )(x2d, g2, w1, w2, gf)


def _suffix_mask(tm):
    t = np.arange(tm)
    m = (t[:, None] // CHUNK == t[None, :] // CHUNK) & (t[None, :] > t[:, None])
    return jnp.asarray(m, dtype=BF16)


def kernel(x, norm1_g, w_in, w_gate_up, b_gate, conv_w, conv_norm_g, gla_norm_g, w_out, norm2_g,
           w_ff1, w_ff2, norm_f_g):
    b, s, d = x.shape
    depth = w_in.shape[0]
    d_conv = conv_w.shape[1]
    d_k = w_gate_up.shape[2]
    d_v = gla_norm_g.shape[1] * GLA_HEADS
    n_main = 3 * d_conv + 2 * d_k + 2 * d_v
    assert w_in.shape[2] == n_main + GATE_RANK
    assert depth == 1, "the final norm is fused into the (single) layer's MLP call"
    us = _suffix_mask(TM_MIX)
    for l in range(depth):
        wt = jnp.swapaxes(w_in, 1, 2).astype(BF16)
        wa_p = jnp.pad(wt[l, n_main:, :], ((0, LANES - GATE_RANK), (0, 0)))
        wg_p = jnp.pad(w_gate_up[l].astype(BF16), ((0, LANES - GATE_RANK), (0, 0)))
        z, za, w1b, w2b, wob = _inproj(x.reshape(b * s, d), norm1_g[l][None, :], wt, wa_p,
                                       w_ff1, w_ff2, w_out, n_main=n_main, layer=l)
        x = _mixer(z.reshape(b, s, -1), za.reshape(b, s, -1), x, wg_p, b_gate[l][None, :],
                   conv_w[l].T, conv_norm_g[l][None, :], gla_norm_g[l][None, :], wob, us,
                   d_conv=d_conv, d_k=d_k, d_v=d_v)
        x = _mlp(x.reshape(b * s, d), norm2_g[l][None, :], w1b, w2b,
                 norm_f_g[None, :]).reshape(b, s, d)
    return x
```
